```python
import numpy as np
import jax
import jax.numpy as jnp
from jax import lax

D_MODEL = 1024
BATCH = 8
SEQ = 2048
DEPTH = 1

RMS_EPS = 1e-6
N_MOD = 6
ROPE_THETA = 500000.0
NEG_INF = -1e30

GLA_HEADS = 4
GLA_DK = D_MODEL // (2 * GLA_HEADS)
GLA_DV = D_MODEL // GLA_HEADS
GLA_RANK = 16
GLA_TAU = 16.0
GLA_CHUNK = 64

NSA_HEADS = 16
NSA_HD = D_MODEL // NSA_HEADS
NSA_GROUPS = 2
NSA_HPG = NSA_HEADS // NSA_GROUPS
ROT_DIM = NSA_HD // 4
CMP_BLOCK = 32
CMP_STRIDE = 16
SEL_BLOCK = 64
SEL_TOPK = 16
WINDOW = 512
ATT_QBLOCK = 128
SEL_QBLOCK = 32
FORCED_SCORE = 1e6

PEER_HEADS = 8
PEER_NKEYS = 128
PEER_NEXPERTS = PEER_NKEYS * PEER_NKEYS
PEER_QDIM = 256
PEER_TOPK = 16
PEER_TOKEN_BLOCK = 128

GLA_QK = GLA_HEADS * GLA_DK
GLA_V = GLA_HEADS * GLA_DV
NSA_Q = NSA_HEADS * NSA_HD
NSA_KV = NSA_GROUPS * NSA_HD
IN_SIZES = (GLA_QK, GLA_QK, GLA_V, GLA_V, GLA_RANK,
            NSA_Q, NSA_KV, NSA_KV, NSA_KV, NSA_KV, NSA_KV, NSA_KV, 3 * NSA_HEADS,
            D_MODEL, D_MODEL)
IN_COLS = sum(IN_SIZES)
IN_OFFSETS = tuple(int(o) for o in np.cumsum(IN_SIZES)[:-1])

kernel_name = 'hybrid_gla_nsa_peer_adaln_block'


def rmsnorm(x, g):
    xf = x.astype(jnp.float32)
    y = xf * lax.rsqrt(jnp.mean(xf * xf, axis=-1, keepdims=True) + RMS_EPS)
    return (y * g.astype(jnp.float32)).astype(x.dtype)


def masked_softmax(s, mask):
    return jax.nn.softmax(jnp.where(mask, s.astype(jnp.float32), NEG_INF), axis=-1)


def rope_partial(t, positions):
    half = ROT_DIM // 2
    inv_freq = jnp.asarray(ROPE_THETA ** (-np.arange(half, dtype=np.float32) * 2.0 / ROT_DIM), dtype=jnp.float32)
    ang = positions.astype(jnp.float32)[..., None] * inv_freq
    cos = jnp.cos(ang)[:, :, None, :]
    sin = jnp.sin(ang)[:, :, None, :]
    tf = t.astype(jnp.float32)
    x1 = tf[..., :half]
    x2 = tf[..., half:ROT_DIM]
    out = jnp.concatenate([x1 * cos - x2 * sin, x2 * cos + x1 * sin, tf[..., ROT_DIM:]], axis=-1)
    return out.astype(t.dtype)


def gla_mixer(q, k, v, r, lr, wa2, ba2, norm_g):
    B, S = q.shape[0], q.shape[1]
    H, DK, DV, C = GLA_HEADS, GLA_DK, GLA_DV, GLA_CHUNK
    nc = S // C
    log_a = jax.nn.log_sigmoid((lr @ wa2 + ba2).astype(jnp.float32)) / GLA_TAU

    def chunks(t, d):
        return t.astype(jnp.float32).reshape(B, nc, C, H, d).transpose(1, 0, 3, 2, 4)

    qs = chunks(q, DK) * (DK ** -0.5)
    kcs = chunks(k, DK)
    vcs = chunks(v, DV)
    acs = chunks(log_a, DK)
    causal = jnp.asarray(np.tril(np.ones((C, C), dtype=bool)))[:, :, None]

    def step(state, xs):
        qc, kc, vc, ac = xs
        b = jnp.cumsum(ac, axis=2)
        diff = b[:, :, :, None, :] - b[:, :, None, :, :]
        decay = jnp.exp(jnp.where(causal, diff, -jnp.inf))
        scores = jnp.einsum('bhtd,bhsd,bhtsd->bhts', qc, kc, decay)
        o = (jnp.einsum('bhts,bhsv->bhtv', scores, vc)
             + jnp.einsum('bhtd,bhdv->bhtv', qc * jnp.exp(b), state))
        b_last = b[:, :, -1:, :]
        state = (jnp.exp(b_last[:, :, 0, :, None]) * state
                 + jnp.einsum('bhsd,bhsv->bhdv', kc * jnp.exp(b_last - b), vc))
        return state, o

    state0 = jnp.zeros((B, H, DK, DV), jnp.float32)
    _, o = lax.scan(step, state0, (qs, kcs, vcs, acs))
    o = o.transpose(1, 0, 3, 2, 4).reshape(B, S, H, DV)
    o = rmsnorm(o, norm_g) * jax.nn.silu(r.astype(jnp.float32).reshape(B, S, H, DV))
    return o.reshape(B, S, H * DV).astype(q.dtype)


def nsa_compressed(q, kc, vc, pe_k, pe_v, ck_w1, ck_w2, cv_w1, cv_w2):
    B, S = q.shape[0], q.shape[1]
    nc = (S - CMP_BLOCK) // CMP_STRIDE + 1
    idx = np.arange(nc)[:, None] * CMP_STRIDE + np.arange(CMP_BLOCK)[None, :]

    def compress(t, pe, w1, w2):
        blk = t[:, idx] + pe[:, None, :]
        blk = blk.transpose(0, 1, 3, 2, 4).reshape(B, nc, NSA_GROUPS, CMP_BLOCK * NSA_HD)
        return jax.nn.gelu(blk @ w1, approximate=False) @ w2

    k_cmp = compress(kc, pe_k, ck_w1, ck_w2)
    v_cmp = compress(vc, pe_v, cv_w1, cv_w2)
    t = np.arange(S)
    blk_end = np.arange(nc) * CMP_STRIDE + CMP_BLOCK - 1
    valid = blk_end[None, :] <= t[:, None]
    has_any = jnp.asarray(valid.any(axis=-1).astype(np.float32))[:, None]
    valid = jnp.asarray(valid)
    s = jnp.einsum('bsghd,bngd->bghsn', q, k_cmp)
    p = masked_softmax(s, valid) * has_any
    o = jnp.einsum('bghsn,bngd->bsghd', p.astype(v_cmp.dtype), v_cmp)
    return o, p.sum(axis=2)


def nsa_select_blocks(imp, S):
    nc = imp.shape[-1]
    ns = S // SEL_BLOCK
    i = np.arange(nc)[:, None]
    j = np.arange(ns)[None, :]
    overlap = ((i * CMP_STRIDE < (j + 1) * SEL_BLOCK) & (i * CMP_STRIDE + CMP_BLOCK > j * SEL_BLOCK))
    score = jnp.einsum('bgsn,nj->bgsj', imp, jnp.asarray(overlap.astype(np.float32)))
    cur = (np.arange(S) // SEL_BLOCK)[:, None]
    forced = jnp.asarray((j == 0) | (j == cur) | (j == cur - 1))
    valid = jnp.asarray(j <= cur)
    score = jnp.where(forced, FORCED_SCORE, jnp.where(valid, score, -1.0))
    _, sel = lax.top_k(score, min(SEL_TOPK, ns))
    return sel


def nsa_selected(q, ks, vs, sel):
    B, S = q.shape[0], q.shape[1]
    G, HPG, HD, L = NSA_GROUPS, NSA_HPG, NSA_HD, SEL_BLOCK
    ns = S // L
    n = sel.shape[-1]
    nq = S // SEL_QBLOCK
    ks_blk = ks.reshape(B, ns, L, G, HD).transpose(0, 3, 1, 2, 4)
    vs_blk = vs.reshape(B, ns, L, G, HD).transpose(0, 3, 1, 2, 4)
    q_ch = q.reshape(B, nq, SEL_QBLOCK, G, HPG, HD).transpose(1, 0, 2, 3, 4, 5)
    sel_ch = sel.reshape(B, G, nq, SEL_QBLOCK, n).transpose(2, 0, 1, 3, 4)
    pos_ch = jnp.arange(S).reshape(nq, SEL_QBLOCK)
    bi = jnp.arange(B)[:, None, None, None]
    gi = jnp.arange(G)[None, :, None, None]

    def body(args):
        qc, ic, tc = args
        kg = ks_blk[bi, gi, ic]
        vg = vs_blk[bi, gi, ic]
        kpos = ic[..., None] * L + jnp.arange(L)
        mask = (kpos <= tc[None, None, :, None, None]).reshape(B, G, 1, SEL_QBLOCK, n * L)
        s = jnp.einsum('bqghd,bgqnld->bghqnl', qc, kg).reshape(B, G, HPG, SEL_QBLOCK, n * L)
        p = masked_softmax(s, mask)
        return jnp.einsum('bghqm,bgqmd->bqghd', p.astype(vg.dtype), vg.reshape(B, G, SEL_QBLOCK, n * L, HD))

    out = lax.map(body, (q_ch, sel_ch, pos_ch))
    return out.transpose(1, 0, 2, 3, 4, 5).reshape(B, S, G, HPG, HD)


def nsa_window(q, kw, vw):
    B, S = q.shape[0], q.shape[1]
    G, HPG, HD, QB = NSA_GROUPS, NSA_HPG, NSA_HD, ATT_QBLOCK
    nb = S // QB
    span = QB + WINDOW
    pad = ((0, 0), (WINDOW, 0), (0, 0), (0, 0))
    kp = jnp.pad(kw, pad)
    vp = jnp.pad(vw, pad)
    q_bl = q.reshape(B, nb, QB, G, HPG, HD).transpose(1, 0, 2, 3, 4, 5)

    def body(args):
        qb, i = args
        start = i * QB
        kb = lax.dynamic_slice_in_dim(kp, start, span, axis=1)
        vb = lax.dynamic_slice_in_dim(vp, start, span, axis=1)
        qpos = start + jnp.arange(QB)
        kpos = start - WINDOW + jnp.arange(span)
        dist = qpos[:, None] - kpos[None, :]
        mask = (dist >= 0) & (dist < WINDOW) & (kpos[None, :] >= 0)
        s = jnp.einsum('bqghd,bkgd->bghqk', qb, kb)
        p = masked_softmax(s, mask)
        return jnp.einsum('bghqk,bkgd->bqghd', p.astype(vb.dtype), vb)

    out = lax.map(body, (q_bl, jnp.arange(nb)))
    return out.transpose(1, 0, 2, 3, 4, 5).reshape(B, S, G, HPG, HD)


def nsa_mixer(q, kc, vc, ks, vs, kw, vw, gates, positions, pe_k, pe_v, ck_w1, ck_w2, cv_w1, cv_w2):
    B, S = q.shape[0], q.shape[1]
    G, HPG, HD = NSA_GROUPS, NSA_HPG, NSA_HD
    q = (rope_partial(q.reshape(B, S, NSA_HEADS, HD), positions) * (HD ** -0.5)).reshape(B, S, G, HPG, HD)
    kc = rope_partial(kc.reshape(B, S, G, HD), positions)
    ks = rope_partial(ks.reshape(B, S, G, HD), positions)
    kw = rope_partial(kw.reshape(B, S, G, HD), positions)
    vc = vc.reshape(B, S, G, HD)
    vs = vs.reshape(B, S, G, HD)
    vw = vw.reshape(B, S, G, HD)
    o_cmp, imp = nsa_compressed(q, kc, vc, pe_k, pe_v, ck_w1, ck_w2, cv_w1, cv_w2)
    sel = nsa_select_blocks(imp, S)
    o_sel = nsa_selected(q, ks, vs, sel)
    o_win = nsa_window(q, kw, vw)
    g = jax.nn.sigmoid(gates.astype(jnp.float32)).reshape(B, S, 3, G, HPG)[..., None]
    o = g[:, :, 0] * o_cmp + g[:, :, 1] * o_sel + g[:, :, 2] * o_win
    return o.reshape(B, S, NSA_HEADS * HD).astype(q.dtype)


def hybrid_mixer(h, positions, w_in, gla_wa2, gla_ba2, gla_norm_g, nsa_pe_k, nsa_pe_v,
                 nsa_ck_w1, nsa_ck_w2, nsa_cv_w1, nsa_cv_w2, w_branch_a, w_branch_b, w_out):
    z = h @ w_in
    (g_q, g_k, g_v, g_r, g_lr, n_q, n_kc, n_vc, n_ks, n_vs, n_kw, n_vw, n_gate,
     merge_a, merge_b) = jnp.split(z, IN_OFFSETS, axis=-1)
    y_a = gla_mixer(g_q, g_k, g_v, g_r, g_lr, gla_wa2, gla_ba2, gla_norm_g)
    y_b = nsa_mixer(n_q, n_kc, n_vc, n_ks, n_vs, n_kw, n_vw, n_gate, positions,
                    nsa_pe_k, nsa_pe_v, nsa_ck_w1, nsa_ck_w2, nsa_cv_w1, nsa_cv_w2)
    m = jax.nn.sigmoid(merge_a) * (y_a @ w_branch_a) + jax.nn.sigmoid(merge_b) * (y_b @ w_branch_b)
    return m @ w_out


def peer_ffn(h, wq, k1, k2, u, v):
    B, S, D = h.shape
    T = B * S
    half = PEER_QDIM // 2
    hf = h.reshape(T, D)
    q = (hf @ wq).reshape(T, PEER_HEADS, PEER_QDIM)
    s1 = jnp.einsum('thd,hnd->thn', q[..., :half], k1).astype(jnp.float32)
    s2 = jnp.einsum('thd,hnd->thn', q[..., half:], k2).astype(jnp.float32)
    v1, i1 = lax.top_k(s1, PEER_TOPK)
    v2, i2 = lax.top_k(s2, PEER_TOPK)
    cand = (v1[..., :, None] + v2[..., None, :]).reshape(T, PEER_HEADS, PEER_TOPK * PEER_TOPK)
    cidx = (i1[..., :, None] * PEER_NKEYS + i2[..., None, :]).reshape(T, PEER_HEADS, PEER_TOPK * PEER_TOPK)
    top_s, pos = lax.top_k(cand, PEER_TOPK)
    eidx = jnp.take_along_axis(cidx, pos, axis=-1)
    gw = jax.nn.softmax(top_s, axis=-1)
    nt = T // PEER_TOKEN_BLOCK

    def body(args):
        hc, ec, gc = args
        a = jax.nn.gelu(jnp.einsum('td,thkd->thk', hc, u[ec]).astype(jnp.float32), approximate=False)
        w = (gc * a).astype(hc.dtype)
        return jnp.einsum('thk,thkd->td', w, v[ec])

    out = lax.map(body, (hf.reshape(nt, PEER_TOKEN_BLOCK, D),
                         eidx.reshape(nt, PEER_TOKEN_BLOCK, PEER_HEADS, PEER_TOPK),
                         gw.reshape(nt, PEER_TOKEN_BLOCK, PEER_HEADS, PEER_TOPK)))
    return out.reshape(B, S, D)


def setup_inputs(seed: int = 0) -> dict:
    key = jax.random.key(seed)
    ks = jax.random.split(key, 32)
    f32 = jnp.float32

    def nrm(k, shape, scale):
        return jax.random.normal(k, shape, f32) * scale

    L, D = DEPTH, D_MODEL
    x = nrm(ks[0], (BATCH, SEQ, D), 1.0)
    c = nrm(ks[1], (BATCH, D), 1.0)
    positions = (jnp.arange(SEQ, dtype=jnp.int32)[None, :]
                 + jax.random.randint(ks[2], (BATCH, 1), 0, 4096, dtype=jnp.int32))
    return {
        'x': x,
        'c': c,
        'positions': positions,
        'ada_w': nrm(ks[3], (L, D, N_MOD * D), 0.5 * D ** -0.5),
        'ada_b': nrm(ks[4], (L, N_MOD * D), 0.02),
        'norm1_g': 1.0 + nrm(ks[5], (L, D), 0.02),
        'norm2_g': 1.0 + nrm(ks[6], (L, D), 0.02),
        'final_g': 1.0 + nrm(ks[7], (D,), 0.02),
        'w_in': nrm(ks[8], (L, D, IN_COLS), D ** -0.5),
        'gla_wa2': nrm(ks[9], (L, GLA_RANK, GLA_QK), GLA_RANK ** -0.5),
        'gla_ba2': nrm(ks[10], (L, GLA_QK), 0.1),
        'gla_norm_g': 1.0 + nrm(ks[11], (L, GLA_DV), 0.02),
        'nsa_pe_k': nrm(ks[12], (L, CMP_BLOCK, NSA_HD), 0.02),
        'nsa_pe_v': nrm(ks[13], (L, CMP_BLOCK, NSA_HD), 0.02),
        'nsa_ck_w1': nrm(ks[14], (L, CMP_BLOCK * NSA_HD, NSA_HD), (CMP_BLOCK * NSA_HD) ** -0.5),
        'nsa_ck_w2': nrm(ks[15], (L, NSA_HD, NSA_HD), NSA_HD ** -0.5),
        'nsa_cv_w1': nrm(ks[16], (L, CMP_BLOCK * NSA_HD, NSA_HD), (CMP_BLOCK * NSA_HD) ** -0.5),
        'nsa_cv_w2': nrm(ks[17], (L, NSA_HD, NSA_HD), NSA_HD ** -0.5),
        'w_branch_a': nrm(ks[18], (L, GLA_V, D), GLA_V ** -0.5),
        'w_branch_b': nrm(ks[19], (L, NSA_Q, D), NSA_Q ** -0.5),
        'w_out': nrm(ks[20], (L, D, D), D ** -0.5),
        'peer_wq': nrm(ks[21], (L, D, PEER_HEADS * PEER_QDIM), D ** -0.5),
        'peer_k1': nrm(ks[22], (L, PEER_HEADS, PEER_NKEYS, PEER_QDIM // 2), (PEER_QDIM // 2) ** -0.5),
        'peer_k2': nrm(ks[23], (L, PEER_HEADS, PEER_NKEYS, PEER_QDIM // 2), (PEER_QDIM // 2) ** -0.5),
        'peer_u': nrm(ks[24], (L, PEER_NEXPERTS, D), D ** -0.5),
        'peer_v': nrm(ks[25], (L, PEER_NEXPERTS, D), 0.1),
    }


def reference(x, c, positions, ada_w, ada_b, norm1_g, norm2_g, final_g, w_in, gla_wa2, gla_ba2,
              gla_norm_g, nsa_pe_k, nsa_pe_v, nsa_ck_w1, nsa_ck_w2, nsa_cv_w1, nsa_cv_w2,
              w_branch_a, w_branch_b, w_out, peer_wq, peer_k1, peer_k2, peer_u, peer_v):
    B, D = c.shape
    for layer in range(DEPTH):
        mod = (jax.nn.silu(c) @ ada_w[layer] + ada_b[layer]).reshape(B, N_MOD, D)
        shift1, scale1, gate1, shift2, scale2, gate2 = [mod[:, i, None, :] for i in range(N_MOD)]
        h = rmsnorm(x, norm1_g[layer]) * (1.0 + scale1) + shift1
        x = x + gate1 * hybrid_mixer(h, positions, w_in[layer], gla_wa2[layer], gla_ba2[layer],
                                     gla_norm_g[layer], nsa_pe_k[layer], nsa_pe_v[layer],
                                     nsa_ck_w1[layer], nsa_ck_w2[layer], nsa_cv_w1[layer], nsa_cv_w2[layer],
                                     w_branch_a[layer], w_branch_b[layer], w_out[layer])
        h = rmsnorm(x, norm2_g[layer]) * (1.0 + scale2) + shift2
        x = x + gate2 * peer_ffn(h, peer_wq[layer], peer_k1[layer], peer_k2[layer], peer_u[layer], peer_v[layer])
    return rmsnorm(x, final_g)
```

```python
import functools

import numpy as np
import jax
import jax.numpy as jnp
from jax import lax
from jax.experimental import pallas as pl
from jax.experimental.pallas import tpu as pltpu

F32 = jnp.float32
BF16 = jnp.bfloat16

RMS_EPS = 1e-6
N_MOD = 6
ROPE_THETA = 500000.0
NEG_INF = -1e30

GLA_HEADS = 4
GLA_RANK = 16
GLA_TAU = 16.0
GLA_CHUNK = 64
GLA_SUB = 16

NSA_HEADS = 16
NSA_GROUPS = 2
NSA_HPG = NSA_HEADS // NSA_GROUPS
NSA_HD = 64
ROT_DIM = NSA_HD // 4
CMP_BLOCK = 32
CMP_STRIDE = 16
SEL_BLOCK = 64
SEL_TOPK = 16
WINDOW = 512
FORCED_SCORE = 1e6

PEER_HEADS = 8
PEER_NKEYS = 128
PEER_TOPK = 16

LANES = 128
VMEM_LIMIT = 56 * 1024 * 1024


def _cparams(sem):
    return pltpu.CompilerParams(dimension_semantics=sem, vmem_limit_bytes=VMEM_LIMIT)


def _dot(a, b):
    return jnp.dot(a, b, preferred_element_type=F32)


def _dot_nt(a, b):
    return lax.dot_general(a, b, (((1,), (1,)), ((), ())), preferred_element_type=F32)


def _dot_tn(a, b):
    return lax.dot_general(a, b, (((0,), (0,)), ((), ())), preferred_element_type=F32)


def _split3(x):
    hi = x.astype(BF16)
    r1 = x - hi.astype(F32)
    mid = r1.astype(BF16)
    lo = (r1 - mid.astype(F32)).astype(BF16)
    return hi, mid, lo


def _dot_exact_rhs(x, m01):
    hi, mid, lo = _split3(x)
    return _dot(hi, m01) + _dot(mid, m01) + _dot(lo, m01)


def _rms(x, g):
    ms = jnp.mean(x * x, axis=-1, keepdims=True)
    return x * lax.rsqrt(ms + RMS_EPS) * g


def _gelu(x):
    return 0.5 * x * (1.0 + lax.erf(x * np.float32(1.0 / np.sqrt(2.0))))


def _log_sigmoid(x):
    return jnp.minimum(x, 0.0) - jnp.log1p(jnp.exp(-jnp.abs(x)))


def _mod_kernel(c_ref, w_ref, b_ref, o_ref):
    c = c_ref[...]
    s = c * jax.nn.sigmoid(c)
    o_ref[...] = jnp.dot(s, w_ref[...], preferred_element_type=F32,
                         precision=lax.Precision.HIGHEST) + b_ref[...]


def _mod(c, ada_w, ada_b):
    B, D = c.shape
    n = ada_w.shape[1]
    tn = 1024
    return pl.pallas_call(
        _mod_kernel,
        out_shape=jax.ShapeDtypeStruct((B, n), F32),
        grid=(n // tn,),
        in_specs=[pl.BlockSpec((B, D), lambda j: (0, 0)),
                  pl.BlockSpec((D, tn), lambda j: (0, j)),
                  pl.BlockSpec((1, tn), lambda j: (0, j))],
        out_specs=pl.BlockSpec((B, tn), lambda j: (0, j)),
        compiler_params=_cparams(("arbitrary",)),
        name="mod",
    )(c, ada_w, ada_b.reshape(1, n))


def _inproj_kernel(x_ref, mod_ref, g_ref, w_ref, o_ref, h_ref):
    @pl.when(pl.program_id(1) == 0)
    def _():
        m = mod_ref[0]
        h = _rms(x_ref[...], g_ref[...]) * (1.0 + m[1:2, :]) + m[0:1, :]
        h_ref[...] = h.astype(BF16)

    o_ref[...] = _dot(h_ref[...], w_ref[...]).astype(BF16)


def _inproj(x2, mod3, g1, w_pad, seq):
    T, D = x2.shape
    NP = w_pad.shape[1]
    tm, tn = 512, 1024
    per_b = seq // tm
    return pl.pallas_call(
        _inproj_kernel,
        out_shape=jax.ShapeDtypeStruct((T, NP), BF16),
        grid=(T // tm, NP // tn),
        in_specs=[pl.BlockSpec((tm, D), lambda i, j: (i, 0)),
                  pl.BlockSpec((1, N_MOD, D), lambda i, j: (i // per_b, 0, 0)),
                  pl.BlockSpec((1, D), lambda i, j: (0, 0)),
                  pl.BlockSpec((D, tn), lambda i, j: (0, j))],
        out_specs=pl.BlockSpec((tm, tn), lambda i, j: (i, j)),
        scratch_shapes=[pltpu.VMEM((tm, D), BF16)],
        compiler_params=_cparams(("parallel", "arbitrary")),
        name="inproj",
    )(x2, mod3, g1, w_pad)


ZB_GQ, ZB_GK, ZB_GV, ZB_GR, ZB_NQ, ZB_MA, ZB_MB = 0, 4, 8, 16, 24, 32, 40
ZB_LR, ZB_KC, ZB_VC, ZB_KS, ZB_VS, ZB_KW, ZB_VW, ZB_GATE = 48, 49, 50, 51, 52, 53, 54, 55
ZB_TOTAL = 56


def _pad_cols(w, width):
    return jnp.pad(w, ((0, 0), (0, width - w.shape[1])))


def _build_w_pad(w_in, D):
    sizes = (D // 2, D // 2, D, D, GLA_RANK, D, 128, 128, 128, 128, 128, 128, 3 * NSA_HEADS, D, D)
    offs = np.concatenate([[0], np.cumsum(sizes)])
    seg = [w_in[:, int(offs[k]):int(offs[k + 1])] for k in range(len(sizes))]
    (g_q, g_k, g_v, g_r, g_lr, n_q, n_kc, n_vc, n_ks, n_vs, n_kw, n_vw, n_gate, m_a, m_b) = seg
    cols = [g_q, g_k, g_v, g_r, n_q, m_a, m_b, _pad_cols(g_lr, LANES),
            n_kc, n_vc, n_ks, n_vs, n_kw, n_vw, _pad_cols(n_gate, LANES)]
    return jnp.concatenate(cols, axis=1).astype(BF16)


def _gla_kernel(q_ref, k_ref, v_ref, r_ref, lr_ref, wa2_ref, ba2_ref, ng_ref, o_ref, st_ref):
    C, SUB = GLA_CHUNK, GLA_SUB
    dk = q_ref.shape[1] // GLA_HEADS
    dv = v_ref.shape[1] // GLA_HEADS

    @pl.when(pl.program_id(1) == 0)
    def _():
        st_ref[...] = jnp.zeros(st_ref.shape, F32)

    xg = _dot(lr_ref[...], wa2_ref[...]) + ba2_ref[...]
    la = _log_sigmoid(xg) * np.float32(1.0 / GLA_TAU)
    row = lax.broadcasted_iota(jnp.int32, (C, C), 0)
    col = lax.broadcasted_iota(jnp.int32, (C, C), 1)
    tri = (col <= row).astype(BF16)
    b_all = _dot_exact_rhs_left(tri, la)
    ng = ng_ref[...]

    for hh in range(GLA_HEADS):
        qh = q_ref[:, hh * dk:(hh + 1) * dk].astype(F32) * np.float32(dk ** -0.5)
        kh = k_ref[:, hh * dk:(hh + 1) * dk].astype(F32)
        vh = v_ref[:, hh * dv:(hh + 1) * dv]
        bh = b_all[:, hh * dk:(hh + 1) * dk]
        st = st_ref[hh]
        o = _dot_nt((qh * jnp.exp(bh)).astype(BF16), st.astype(BF16))
        parts = []
        for i in range(C // SUB):
            r0, r1 = i * SUB, (i + 1) * SUB
            bref = bh[r0 - 1:r0, :] if i > 0 else jnp.zeros((1, dk), F32)
            qi = (qh[r0:r1] * jnp.exp(bh[r0:r1] - bref)).astype(BF16)
            ka = (kh[0:r1] * jnp.exp(bref - bh[0:r1])).astype(BF16)
            a = _dot_nt(qi, ka)
            ti = lax.broadcasted_iota(jnp.int32, (SUB, r1), 0) + r0
            si = lax.broadcasted_iota(jnp.int32, (SUB, r1), 1)
            a = jnp.where(si <= ti, a, 0.0)
            parts.append(_dot(a.astype(BF16), vh[0:r1]))
        o = o + jnp.concatenate(parts, axis=0)
        blast = bh[C - 1:C, :]
        kd = (kh * jnp.exp(blast - bh)).astype(BF16)
        st_ref[hh] = st * jnp.exp(blast) + _dot_tn(vh, kd)
        r = r_ref[:, hh * dv:(hh + 1) * dv].astype(F32)
        y = _rms(o, ng) * (r * jax.nn.sigmoid(r))
        o_ref[:, hh * dv:(hh + 1) * dv] = y.astype(BF16)


def _dot_exact_rhs_left(m01, x):
    hi, mid, lo = _split3(x)
    return _dot(m01, hi) + _dot(m01, mid) + _dot(m01, lo)


def _gla(z, wa2p, ba2, ng, batch, seq, D):
    T = z.shape[0]
    C = GLA_CHUNK
    nc = seq // C
    dkt, dvt = D // 2, D
    rows = lambda b, c: b * nc + c
    return pl.pallas_call(
        _gla_kernel,
        out_shape=jax.ShapeDtypeStruct((T, dvt), BF16),
        grid=(batch, nc),
        in_specs=[pl.BlockSpec((C, dkt), lambda b, c: (rows(b, c), 0)),
                  pl.BlockSpec((C, dkt), lambda b, c: (rows(b, c), 1)),
                  pl.BlockSpec((C, dvt), lambda b, c: (rows(b, c), 1)),
                  pl.BlockSpec((C, dvt), lambda b, c: (rows(b, c), 2)),
                  pl.BlockSpec((C, LANES), lambda b, c: (rows(b, c), ZB_LR)),
                  pl.BlockSpec((LANES, dkt), lambda b, c: (0, 0)),
                  pl.BlockSpec((1, dkt), lambda b, c: (0, 0)),
                  pl.BlockSpec((1, dvt // GLA_HEADS), lambda b, c: (0, 0))],
        out_specs=pl.BlockSpec((C, dvt), lambda b, c: (rows(b, c), 0)),
        scratch_shapes=[pltpu.VMEM((GLA_HEADS, dvt // GLA_HEADS, dkt // GLA_HEADS), F32)],
        compiler_params=_cparams(("parallel", "arbitrary")),
        name="gla",
    )(z, z, z, z, z, wa2p, ba2, ng)


def _rope_consts():
    half = ROT_DIM // 2
    inv_freq = (ROPE_THETA ** (-np.arange(half, dtype=np.float32) * 2.0 / ROT_DIM)).astype(np.float32)
    lane = np.arange(LANES)
    within = lane % NSA_HD
    invf = np.where(within < ROT_DIM, inv_freq[within % half], 0.0).astype(np.float32)
    sign = np.where(within < half, -1.0, np.where(within < ROT_DIM, 1.0, 0.0)).astype(np.float32)
    perm = np.zeros((LANES, LANES), np.float32)
    for l in range(LANES):
        w = l % NSA_HD
        if w < half:
            perm[l + half, l] = 1.0
        elif w < ROT_DIM:
            perm[l - half, l] = 1.0
    return invf.reshape(1, LANES), sign.reshape(1, LANES), perm


def _rope_kernel(pos_ref, invf_ref, sign_ref, perm_ref, q_ref, kc_ref, vc_ref, ks_ref, vs_ref, kw_ref, vw_ref,
                 qo_ref, kco_ref, kk_ref, vv_ref):
    ang = pos_ref[...].astype(F32) * invf_ref[...]
    cosf = jnp.cos(ang)
    sinf = jnp.sin(ang) * sign_ref[...]
    perm = perm_ref[...]

    def rope(t):
        return t.astype(F32) * cosf + _dot(t, perm) * sinf

    for j in range(q_ref.shape[1] // LANES):
        sl = slice(j * LANES, (j + 1) * LANES)
        qo_ref[:, sl] = (rope(q_ref[:, sl]) * np.float32(NSA_HD ** -0.5)).astype(BF16)
    kco_ref[...] = rope(kc_ref[...])
    del vc_ref
    ks = rope(ks_ref[...]).astype(BF16)
    kw = rope(kw_ref[...]).astype(BF16)
    vs = vs_ref[...]
    vw = vw_ref[...]
    for g in range(NSA_GROUPS):
        sl = slice(g * NSA_HD, (g + 1) * NSA_HD)
        kk_ref[0, 0, g] = ks[:, sl]
        kk_ref[1, 0, g] = kw[:, sl]
        vv_ref[0, 0, g] = vs[:, sl]
        vv_ref[1, 0, g] = vw[:, sl]


def _rope(z, pos2, batch, seq, D):
    T = z.shape[0]
    tm = 512
    per_b = seq // tm
    invf, sign, perm = _rope_consts()
    zb = lambda blk: pl.BlockSpec((tm, LANES), lambda i: (i, blk))
    const = lambda shape: pl.BlockSpec(shape, lambda i: (0,) * len(shape))
    kv_shape = (2, batch, NSA_GROUPS, seq, NSA_HD)
    kv_spec = pl.BlockSpec((2, 1, NSA_GROUPS, tm, NSA_HD), lambda i: (0, i // per_b, 0, i % per_b, 0))
    return pl.pallas_call(
        _rope_kernel,
        out_shape=(jax.ShapeDtypeStruct((T, D), BF16),
                   jax.ShapeDtypeStruct((T, LANES), F32),
                   jax.ShapeDtypeStruct(kv_shape, BF16),
                   jax.ShapeDtypeStruct(kv_shape, BF16)),
        grid=(T // tm,),
        in_specs=[pl.BlockSpec((tm, 1), lambda i: (i, 0)),
                  const((1, LANES)), const((1, LANES)), const((LANES, LANES)),
                  pl.BlockSpec((tm, D), lambda i: (i, ZB_NQ * LANES // D)),
                  zb(ZB_KC), zb(ZB_VC), zb(ZB_KS), zb(ZB_VS), zb(ZB_KW), zb(ZB_VW)],
        out_specs=(pl.BlockSpec((tm, D), lambda i: (i, 0)),
                   pl.BlockSpec((tm, LANES), lambda i: (i, 0)),
                   kv_spec, kv_spec),
        compiler_params=_cparams(("parallel",)),
        name="rope",
    )(pos2, jnp.asarray(invf), jnp.asarray(sign), jnp.asarray(perm, BF16), z, z, z, z, z, z, z)


def _compress_kernel(kc_ref, vc_ref, pek_ref, pev_ref, w1k_ref, w1v_ref, w2k_ref, w2v_ref,
                     ko_ref, vo_ref, buf_ref, cat_ref):
    seq = kc_ref.shape[0]
    nblk = cat_ref.shape[0]

    def run(src, pe_ref, w1_ref, w2_ref, out_ref):
        buf_ref[0:seq, :] = src
        buf_ref[seq:seq + CMP_BLOCK, :] = jnp.zeros((CMP_BLOCK, LANES), F32)
        for l in range(CMP_BLOCK):
            cat_ref[:, l * LANES:(l + 1) * LANES] = buf_ref[pl.ds(l, nblk, stride=CMP_STRIDE), :]
        xb = (cat_ref[...] + pe_ref[...]).astype(BF16)
        hid = _gelu(_dot(xb, w1_ref[...]))
        res = _dot(hid.astype(BF16), w2_ref[...]).astype(BF16)
        for g in range(NSA_GROUPS):
            out_ref[0, g] = res[:, g * NSA_HD:(g + 1) * NSA_HD]

    run(kc_ref[...], pek_ref, w1k_ref, w2k_ref, ko_ref)
    run(vc_ref[...].astype(F32), pev_ref, w1v_ref, w2v_ref, vo_ref)


def _blockdiag_w1(w1):
    w = w1.reshape(CMP_BLOCK, NSA_HD, NSA_HD)
    zero = jnp.zeros_like(w)
    top = jnp.concatenate([w, zero], axis=2)
    bot = jnp.concatenate([zero, w], axis=2)
    return jnp.concatenate([top, bot], axis=1).reshape(CMP_BLOCK * LANES, LANES).astype(BF16)


def _blockdiag_w2(w2):
    zero = jnp.zeros_like(w2)
    return jnp.concatenate([jnp.concatenate([w2, zero], axis=1),
                            jnp.concatenate([zero, w2], axis=1)], axis=0).astype(BF16)


def _pe_big(pe):
    return jnp.concatenate([pe, pe], axis=1).reshape(1, CMP_BLOCK * LANES)


def _compress(kcr, z, pe_k, pe_v, w1k, w2k, w1v, w2v, batch, seq):
    nblk = seq // CMP_STRIDE
    const = lambda shape: pl.BlockSpec(shape, lambda b: (0,) * len(shape))
    out_shape = jax.ShapeDtypeStruct((batch, NSA_GROUPS, nblk, NSA_HD), BF16)
    out_spec = pl.BlockSpec((1, NSA_GROUPS, nblk, NSA_HD), lambda b: (b, 0, 0, 0))
    kw = CMP_BLOCK * LANES
    return pl.pallas_call(
        _compress_kernel,
        out_shape=(out_shape, out_shape),
        grid=(batch,),
        in_specs=[pl.BlockSpec((seq, LANES), lambda b: (b, 0)),
                  pl.BlockSpec((seq, LANES), lambda b: (b, ZB_VC)),
                  const((1, kw)), const((1, kw)), const((kw, LANES)), const((kw, LANES)),
                  const((LANES, LANES)), const((LANES, LANES))],
        out_specs=(out_spec, out_spec),
        scratch_shapes=[pltpu.VMEM((seq + CMP_BLOCK, LANES), F32), pltpu.VMEM((nblk, kw), F32)],
        compiler_params=_cparams(("parallel",)),
        name="compress",
    )(kcr, z, _pe_big(pe_k), _pe_big(pe_v), _blockdiag_w1(w1k), _blockdiag_w1(w1v),
      _blockdiag_w2(w2k), _blockdiag_w2(w2v))


NSA_TQ = 128
NSA_KT = 512


def _nsa_consts(seq):
    nblk = seq // CMP_STRIDE
    ncmp = (seq - CMP_BLOCK) // CMP_STRIDE + 1
    ns = seq // SEL_BLOCK
    n = np.arange(nblk)[:, None]
    j = np.arange(LANES)[None, :]
    ov = ((n * CMP_STRIDE < (j + 1) * SEL_BLOCK) & (n * CMP_STRIDE + CMP_BLOCK > j * SEL_BLOCK)
          & (j < ns) & (n < ncmp)).astype(np.float32)
    expand = (np.arange(seq)[None, :] // SEL_BLOCK == np.arange(LANES)[:, None]).astype(np.float32)
    width = NSA_HPG * NSA_HD
    eg = np.zeros((NSA_GROUPS, LANES, 3 * width), np.float32)
    for g in range(NSA_GROUPS):
        for c in range(3):
            for h in range(NSA_HPG):
                eg[g, c * NSA_HEADS + g * NSA_HPG + h, c * width + h * NSA_HD:c * width + (h + 1) * NSA_HD] = 1.0
    return ov, expand, eg


def _nsa_kernel(q_ref, kc_ref, vc_ref, kk_ref, vv_ref, gate_ref, ov_ref, exp_ref, eg_ref, o_ref, selm_ref):
    tq, H, hd, KT = NSA_TQ, NSA_HPG, NSA_HD, NSA_KT
    seq = kk_ref.shape[3]
    ns = seq // SEL_BLOCK
    i = pl.program_id(2)
    q = q_ref[...]
    qs = jnp.concatenate([q[:, h * hd:(h + 1) * hd] for h in range(H)], axis=0)
    tpos = i * tq + lax.broadcasted_iota(jnp.int32, (1, tq, 1), 1)

    def softmax_rows(s):
        m = jnp.max(s, axis=-1, keepdims=True)
        p = jnp.exp(s - m)
        return p / jnp.sum(p, axis=-1, keepdims=True)

    kc = kc_ref[0, 0]
    nb = kc.shape[0]
    s = _dot_nt(qs, kc).reshape(H, tq, nb)
    nn = lax.broadcasted_iota(jnp.int32, (1, 1, nb), 2)
    s = jnp.where(nn * CMP_STRIDE + (CMP_BLOCK - 1) <= tpos, s, NEG_INF)
    p = jnp.where(tpos >= CMP_BLOCK - 1, softmax_rows(s), 0.0)
    o_cmp = _dot(p.reshape(H * tq, nb).astype(BF16), vc_ref[0, 0])
    imp = jnp.sum(p, axis=0)

    sc = _dot_exact_rhs(imp, ov_ref[...])
    jj = lax.broadcasted_iota(jnp.int32, (1, LANES), 1)
    t2 = i * tq + lax.broadcasted_iota(jnp.int32, (tq, 1), 0)
    cur = lax.shift_right_logical(t2, int(np.log2(SEL_BLOCK)))
    forced = (jj == 0) | (jj == cur) | (jj == cur - 1)
    sc = jnp.where(forced, FORCED_SCORE, jnp.where(jj <= cur, sc, -1.0))
    sc = jnp.where(jj < ns, sc, -2.0)
    rank = jnp.zeros((tq, LANES), jnp.int32)
    for j2 in range(ns):
        colv = sc[:, j2:j2 + 1]
        beats = (colv > sc) | ((colv == sc) & (jj > j2))
        rank = rank + beats.astype(jnp.int32)
    sel = (rank < min(SEL_TOPK, ns)).astype(BF16)
    selm = _dot(sel, exp_ref[...])
    for kt in range(seq // KT):
        selm_ref[kt] = selm[:, kt * KT:(kt + 1) * KT]

    def sel_body(kt, carry):
        m, l, acc = carry
        start = pl.multiple_of(kt * KT, KT)
        k = kk_ref[0, 0, 0, pl.ds(start, KT), :]
        v = vv_ref[0, 0, 0, pl.ds(start, KT), :]
        s = _dot_nt(qs, k).reshape(H, tq, KT)
        kpos = start + lax.broadcasted_iota(jnp.int32, (1, 1, KT), 2)
        msk = (selm_ref[kt][None] > 0.5) & (kpos <= tpos)
        s = jnp.where(msk, s, NEG_INF)
        m_new = jnp.maximum(m, jnp.max(s, axis=-1, keepdims=True))
        alpha = jnp.exp(m - m_new)
        p = jnp.where(msk, jnp.exp(s - m_new), 0.0)
        l = alpha * l + jnp.sum(p, axis=-1, keepdims=True)
        pv = _dot(p.reshape(H * tq, KT).astype(BF16), v).reshape(H, tq, hd)
        return m_new, l, alpha * acc + pv

    nkt = (i * tq + tq + KT - 1) // KT
    init = (jnp.full((H, tq, 1), NEG_INF, F32), jnp.zeros((H, tq, 1), F32), jnp.zeros((H, tq, hd), F32))
    _, l, acc = lax.fori_loop(0, nkt, sel_body, init)
    o_sel = (acc / l).reshape(H * tq, hd)

    span = tq + WINDOW
    start = pl.multiple_of(jnp.maximum(i * tq - WINDOW, 0), tq)
    s = _dot_nt(qs, kk_ref[1, 0, 0, pl.ds(start, span), :]).reshape(H, tq, span)
    dist = tpos - (start + lax.broadcasted_iota(jnp.int32, (1, 1, span), 2))
    s = jnp.where((dist >= 0) & (dist < WINDOW), s, NEG_INF)
    o_win = _dot(softmax_rows(s).reshape(H * tq, span).astype(BF16), vv_ref[1, 0, 0, pl.ds(start, span), :])

    def unstack(o):
        return jnp.concatenate([o[h * tq:(h + 1) * tq] for h in range(H)], axis=1)

    width = H * hd
    ge = _dot_exact_rhs(jax.nn.sigmoid(gate_ref[...].astype(F32)), eg_ref[0])
    out = (ge[:, 0:width] * unstack(o_cmp) + ge[:, width:2 * width] * unstack(o_sel)
           + ge[:, 2 * width:3 * width] * unstack(o_win))
    o_ref[...] = out.astype(BF16)


def _nsa(qr, kcmp, vcmp, kk, vv, z, batch, seq, D):
    T = qr.shape[0]
    tq, KT = NSA_TQ, NSA_KT
    nq = seq // tq
    width = NSA_HPG * NSA_HD
    ov, expand, eg = _nsa_consts(seq)
    nblk = kcmp.shape[2]
    row = lambda b, g, i: b * nq + i
    cmp_spec = pl.BlockSpec((1, 1, nblk, NSA_HD), lambda b, g, i: (b, g, 0, 0))
    kv_spec = pl.BlockSpec((2, 1, 1, seq, NSA_HD), lambda b, g, i: (0, b, g, 0, 0))
    return pl.pallas_call(
        _nsa_kernel,
        out_shape=jax.ShapeDtypeStruct((T, D), BF16),
        grid=(batch, NSA_GROUPS, nq),
        in_specs=[pl.BlockSpec((tq, width), lambda b, g, i: (row(b, g, i), g)),
                  cmp_spec, cmp_spec, kv_spec, kv_spec,
                  pl.BlockSpec((tq, LANES), lambda b, g, i: (row(b, g, i), ZB_GATE)),
                  pl.BlockSpec((nblk, LANES), lambda b, g, i: (0, 0)),
                  pl.BlockSpec((LANES, seq), lambda b, g, i: (0, 0)),
                  pl.BlockSpec((1, LANES, 3 * width), lambda b, g, i: (g, 0, 0))],
        out_specs=pl.BlockSpec((tq, width), lambda b, g, i: (row(b, g, i), g)),
        scratch_shapes=[pltpu.VMEM((seq // KT, tq, KT), F32)],
        compiler_params=_cparams(("parallel", "parallel", "arbitrary")),
        name="nsa",
    )(qr, kcmp, vcmp, kk, vv, z, jnp.asarray(ov, BF16), jnp.asarray(expand, BF16), jnp.asarray(eg, BF16))


def _merge_kernel(ya_ref, yb_ref, ma_ref, mb_ref, x_ref, mod_ref, g2_ref, wa_ref, wb_ref, wo_ref, wq_ref,
                  x1_ref, h2t_ref, qp_ref):
    m = mod_ref[0]
    pa = _dot(ya_ref[...], wa_ref[...])
    pb = _dot(yb_ref[...], wb_ref[...])
    mix = (jax.nn.sigmoid(ma_ref[...].astype(F32)) * pa + jax.nn.sigmoid(mb_ref[...].astype(F32)) * pb)
    x1 = x_ref[...] + m[2:3, :] * _dot(mix.astype(BF16), wo_ref[...])
    x1_ref[...] = x1
    h2 = _rms(x1, g2_ref[...]) * (1.0 + m[4:5, :]) + m[3:4, :]
    h2t_ref[...] = h2.T.astype(BF16)
    qp_ref[...] = _dot(h2.astype(BF16), wq_ref[...]).astype(BF16)


def _merge(ya, yb, z, x2, mod3, g2, wa, wb, wo, wq, seq):
    T, D = x2.shape
    nq = wq.shape[1]
    tm = 256
    per_b = seq // tm
    row = lambda blk: pl.BlockSpec((tm, D), lambda i: (i, blk))
    const = lambda shape: pl.BlockSpec(shape, lambda i: (0,) * len(shape))
    return pl.pallas_call(
        _merge_kernel,
        out_shape=(jax.ShapeDtypeStruct((T, D), F32),
                   jax.ShapeDtypeStruct((D, T), BF16),
                   jax.ShapeDtypeStruct((T, nq), BF16)),
        grid=(T // tm,),
        in_specs=[row(0), row(0), row(ZB_MA * LANES // D), row(ZB_MB * LANES // D), row(0),
                  pl.BlockSpec((1, N_MOD, D), lambda i: (i // per_b, 0, 0)),
                  const((1, D)), const((D, D)), const((D, D)), const((D, D)), const((D, nq))],
        out_specs=(pl.BlockSpec((tm, D), lambda i: (i, 0)),
                   pl.BlockSpec((D, tm), lambda i: (0, i)),
                   pl.BlockSpec((tm, nq), lambda i: (i, 0))),
        compiler_params=_cparams(("parallel",)),
        name="merge",
    )(ya, yb, z, z, x2, mod3, g2, wa, wb, wo, wq)


PEER_NEED = PEER_TOPK + 1


def _extract_top(s, count):
    n = s.shape[0]
    ridx = lax.broadcasted_iota(jnp.int32, (n, 1), 0)
    vals = []
    for r in range(count):
        mx = jnp.max(s, axis=0, keepdims=True)
        vals.append(mx)
        if r + 1 < count:
            first = jnp.min(jnp.where(s == mx, ridx, n), axis=0, keepdims=True)
            s = jnp.where(ridx == first, -jnp.inf, s)
    return vals


def _peer_topk_kernel(qp_ref, k1_ref, k2_ref, s2_ref, e2_ref, e1_ref, th_ref):
    half = k1_ref.shape[2]
    for h in range(PEER_HEADS):
        q1 = qp_ref[:, (2 * h) * half:(2 * h + 1) * half]
        q2 = qp_ref[:, (2 * h + 1) * half:(2 * h + 2) * half]
        s1 = _dot_nt(k1_ref[h], q1)
        s2 = _dot_nt(k2_ref[h], q2)
        v1 = _extract_top(s1, PEER_NEED)
        v2 = _extract_top(s2, PEER_NEED)
        cand = [v1[a] + v2[b] for a in range(PEER_NEED) for b in range(PEER_NEED)
                if (a + 1) * (b + 1) <= PEER_NEED]
        pad = (-len(cand)) % 8
        cand = jnp.concatenate(cand + [jnp.full_like(cand[0], -jnp.inf)] * pad, axis=0)
        top = _extract_top(cand, PEER_NEED)
        zsum = jnp.zeros_like(top[0])
        for c in top[:PEER_TOPK]:
            zsum = zsum + jnp.exp(c - top[0])
        tau = 0.5 * (top[PEER_TOPK - 1] + top[PEER_TOPK])
        s2_ref[h] = s2
        e2_ref[h] = jnp.exp(s2 - v2[0])
        e1_ref[h] = jnp.exp(s1 - v1[0]) / zsum
        th_ref[h] = tau - s1


def _peer_topk(qp, k1, k2):
    T, nq = qp.shape
    tt = 256
    half = k1.shape[2]
    shp = jax.ShapeDtypeStruct((PEER_HEADS, PEER_NKEYS, T), F32)
    spec = pl.BlockSpec((PEER_HEADS, PEER_NKEYS, tt), lambda i: (0, 0, i))
    kspec = pl.BlockSpec((PEER_HEADS, PEER_NKEYS, half), lambda i: (0, 0, 0))
    return pl.pallas_call(
        _peer_topk_kernel,
        out_shape=(shp, shp, shp, shp),
        grid=(T // tt,),
        in_specs=[pl.BlockSpec((tt, nq), lambda i: (i, 0)), kspec, kspec],
        out_specs=(spec, spec, spec, spec),
        compiler_params=_cparams(("parallel",)),
        name="peer_topk",
    )(qp, k1, k2)


PEER_TT = 512
PEER_ROWS = 8


def _peer_ffn_kernel(h2t_ref, u_ref, vt_ref, s2_ref, e2_ref, e1_ref, th_ref, x1_ref, mod_ref, gf_ref,
                     o_ref, acc_ref, wg_ref):
    e = pl.program_id(1)
    nk = PEER_NKEYS

    @pl.when(e == 0)
    def _():
        acc_ref[...] = jnp.zeros(acc_ref.shape, F32)

    at = _dot(u_ref[...], h2t_ref[...])
    for il in range(PEER_ROWS):
        w = None
        for h in range(PEER_HEADS):
            contrib = jnp.where(s2_ref[h] > th_ref[h, il:il + 1, :], e2_ref[h] * e1_ref[h, il:il + 1, :], 0.0)
            w = contrib if w is None else w + contrib
        wg_ref[il * nk:(il + 1) * nk, :] = (w * _gelu(at[il * nk:(il + 1) * nk, :])).astype(BF16)
    acc_ref[...] += _dot(vt_ref[...], wg_ref[...])

    @pl.when(e == pl.num_programs(1) - 1)
    def _():
        m = mod_ref[0]
        x2 = x1_ref[...] + m[5:6, :] * acc_ref[...].T
        o_ref[...] = _rms(x2, gf_ref[...])


def _peer_ffn(h2t, u_bf, vt_bf, s2, e2, e1, th, x1, mod3, gf, seq):
    D, T = h2t.shape
    E = u_bf.shape[0]
    tt = PEER_TT
    te = PEER_ROWS * PEER_NKEYS
    per_b = seq // tt
    big = pl.BlockSpec((PEER_HEADS, PEER_NKEYS, tt), lambda i, e: (0, 0, i))
    small = pl.BlockSpec((PEER_HEADS, PEER_ROWS, tt), lambda i, e: (0, e, i))
    return pl.pallas_call(
        _peer_ffn_kernel,
        out_shape=jax.ShapeDtypeStruct((T, D), F32),
        grid=(T // tt, E // te),
        in_specs=[pl.BlockSpec((D, tt), lambda i, e: (0, i)),
                  pl.BlockSpec((te, D), lambda i, e: (e, 0)),
                  pl.BlockSpec((D, te), lambda i, e: (0, e)),
                  big, big, small, small,
                  pl.BlockSpec((tt, D), lambda i, e: (i, 0)),
                  pl.BlockSpec((1, N_MOD, D), lambda i, e: (i // per_b, 0, 0)),
                  pl.BlockSpec((1, D), lambda i, e: (0, 0))],
        out_specs=pl.BlockSpec((tt, D), lambda i, e: (i, 0)),
        scratch_shapes=[pltpu.VMEM((D, tt), F32), pltpu.VMEM((te, tt), BF16)],
        compiler_params=_cparams(("parallel", "arbitrary")),
        name="peer_ffn",
    )(h2t, u_bf, vt_bf, s2, e2, e1, th, x1, mod3, gf)


def kernel(x, c, positions, ada_w, ada_b, norm1_g, norm2_g, final_g, w_in, gla_wa2, gla_ba2, gla_norm_g,
           nsa_pe_k, nsa_pe_v, nsa_ck_w1, nsa_ck_w2, nsa_cv_w1, nsa_cv_w2, w_branch_a, w_branch_b, w_out,
           peer_wq, peer_k1, peer_k2, peer_u, peer_v):
    B, S, D = x.shape
    T = B * S
    depth = ada_w.shape[0]
    xs = x.reshape(T, D)
    pos2 = positions.reshape(T, 1)
    for l in range(depth):
        mod3 = _mod(c, ada_w[l], ada_b[l]).reshape(B, N_MOD, D)
        z = _inproj(xs, mod3, norm1_g[l].reshape(1, D), _build_w_pad(w_in[l], D), S)
        wa2p = jnp.pad(gla_wa2[l], ((0, LANES - GLA_RANK), (0, 0))).astype(BF16)
        ya = _gla(z, wa2p, gla_ba2[l].reshape(1, -1), gla_norm_g[l].reshape(1, -1), B, S, D)
        qr, kcr, kk, vv = _rope(z, pos2, B, S, D)
        kcmp, vcmp = _compress(kcr, z, nsa_pe_k[l], nsa_pe_v[l], nsa_ck_w1[l], nsa_ck_w2[l],
                               nsa_cv_w1[l], nsa_cv_w2[l], B, S)
        yb = _nsa(qr, kcmp, vcmp, kk, vv, z, B, S, D)
        x1, h2t, qp = _merge(ya, yb, z, xs, mod3, norm2_g[l].reshape(1, D), w_branch_a[l].astype(BF16),
                             w_branch_b[l].astype(BF16), w_out[l].astype(BF16), peer_wq[l].astype(BF16), S)
        s2, e2, e1, th = _peer_topk(qp, peer_k1[l].astype(BF16), peer_k2[l].astype(BF16))
        assert depth == 1, "the final norm is fused into the last PEER step; stacking needs an unfused variant"
        xs = _peer_ffn(h2t, peer_u[l].astype(BF16), peer_v[l].T.astype(BF16), s2, e2, e1, th, x1, mod3,
                       final_g.reshape(1, D), S)
    return xs.reshape(B, S, D)
```
